```python
import math
import jax, jax.numpy as jnp
from jax import lax
import numpy as np

D_MODEL = 4096
BATCH = 4
SEQ = 2048
DEPTH = 1

MIX_WIDTH = D_MODEL
GDN_HEADS = 16
GDN_HEAD_DIM = 128
GDN_WIDTH = GDN_HEADS * GDN_HEAD_DIM
GDN_CONV = 4
GDN_CHUNK = 64
CONV_GROUPS = 16
CONV_GROUP_DIM = (MIX_WIDTH - GDN_WIDTH) // CONV_GROUPS
CONV_WIDTH = CONV_GROUPS * CONV_GROUP_DIM
SHORT_CONV = 3
IN_SIZES = [GDN_WIDTH, GDN_WIDTH, GDN_WIDTH,
            GDN_WIDTH,
            GDN_HEADS, GDN_HEADS,
            CONV_WIDTH, CONV_WIDTH, CONV_WIDTH]
IN_COLS = sum(IN_SIZES)
IN_SPLITS = np.cumsum(IN_SIZES)[:-1].tolist()

PEER_HEADS = 8
PEER_TOPK = 16
PEER_NKEYS = 128
PEER_EXPERTS = PEER_NKEYS * PEER_NKEYS
PEER_QDIM = 256
PEER_HALF = PEER_QDIM // 2
PEER_BLOCK = 128

DN_ALPHA = (2.0 * DEPTH) ** 0.25
DN_BETA = (8.0 * DEPTH) ** -0.25
LN_EPS = 1e-5
RMS_EPS = 1e-6

kernel_name = "hybrid_gdn_shortconv_peer_deepnorm"


def layer_norm(x, g, b):
    xf = x.astype(jnp.float32)
    mu = jnp.mean(xf, axis=-1, keepdims=True)
    var = jnp.mean(jnp.square(xf - mu), axis=-1, keepdims=True)
    y = (xf - mu) * lax.rsqrt(var + LN_EPS) * g.astype(jnp.float32) + b.astype(jnp.float32)
    return y.astype(x.dtype)


def l2_normalize(x):
    return x * lax.rsqrt(jnp.sum(jnp.square(x), axis=-1, keepdims=True) + RMS_EPS)


def causal_depthwise_conv(x, w):
    width = w.shape[0]
    seq = x.shape[1]
    xp = jnp.pad(x, ((0, 0), (width - 1, 0), (0, 0)))
    y = w[0] * xp[:, 0:seq]
    for j in range(1, width):
        y = y + w[j] * xp[:, j:j + seq]
    return y


def gated_delta_rule_chunked(q, k, v, g, beta):
    bsz, seq, heads, dk = q.shape
    dv = v.shape[-1]
    c = GDN_CHUNK
    n = seq // c
    q = l2_normalize(q) * (dk ** -0.5)
    k = l2_normalize(k)

    def to_chunks(t):
        return t.reshape(bsz, n, c, heads, -1).transpose(0, 3, 1, 2, 4)

    q, k, v = to_chunks(q), to_chunks(k), to_chunks(v)
    g = g.reshape(bsz, n, c, heads).transpose(0, 3, 1, 2)
    beta = beta.reshape(bsz, n, c, heads).transpose(0, 3, 1, 2)
    g = jnp.cumsum(g, axis=-1)

    causal = jnp.tril(jnp.ones((c, c), dtype=bool))
    strict = jnp.tril(jnp.ones((c, c), dtype=bool), -1)
    diff = g[..., :, None] - g[..., None, :]
    decay = jnp.where(causal, jnp.exp(jnp.where(causal, diff, 0.0)), 0.0)

    k_beta = k * beta[..., None]
    v_beta = v * beta[..., None]
    m = jnp.where(strict, jnp.einsum('bhnik,bhnjk->bhnij', k_beta, k) * decay, 0.0)
    rhs = jnp.concatenate([v_beta, k_beta * jnp.exp(g)[..., None]], axis=-1)
    sol = lax.linalg.triangular_solve(m, rhs, left_side=True, lower=True, unit_diagonal=True)
    u = sol[..., :dv]
    w = sol[..., dv:]
    qk = jnp.where(causal, jnp.einsum('bhnik,bhnjk->bhnij', q, k) * decay, 0.0)

    def step(state, inp):
        q_c, k_c, u_c, w_c, qk_c, g_c = inp
        v_new = u_c - jnp.einsum('bhck,bhkv->bhcv', w_c, state)
        o = (jnp.einsum('bhck,bhkv->bhcv', q_c * jnp.exp(g_c)[..., None], state)
             + jnp.einsum('bhij,bhjv->bhiv', qk_c, v_new))
        g_last = g_c[..., -1]
        k_dec = k_c * jnp.exp(g_last[..., None] - g_c)[..., None]
        state = state * jnp.exp(g_last)[..., None, None] + jnp.einsum('bhck,bhcv->bhkv', k_dec, v_new)
        return state, o

    xs = tuple(jnp.moveaxis(t, 2, 0) for t in (q, k, u, w, qk, g))
    state0 = jnp.zeros((bsz, heads, dk, dv), jnp.float32)
    _, o = lax.scan(step, state0, xs)
    return o.transpose(1, 0, 3, 2, 4).reshape(bsz, seq, heads, dv)


def hybrid_mixer(x, w_in, conv_qkv, a_log, dt_bias, gdn_norm_w, conv_short, w_out):
    bsz, seq, _ = x.shape
    proj = x @ w_in
    q, k, v, z, beta_raw, a_raw, gate_b, gate_c, h = jnp.split(proj, IN_SPLITS, axis=-1)

    qkv = jax.nn.silu(causal_depthwise_conv(jnp.concatenate([q, k, v], axis=-1), conv_qkv))
    q, k, v = jnp.split(qkv.astype(jnp.float32), 3, axis=-1)
    hd = (bsz, seq, GDN_HEADS, GDN_HEAD_DIM)
    beta = jax.nn.sigmoid(beta_raw.astype(jnp.float32))
    g = -jnp.exp(a_log.astype(jnp.float32)) * jax.nn.softplus(
        a_raw.astype(jnp.float32) + dt_bias.astype(jnp.float32))
    o = gated_delta_rule_chunked(q.reshape(hd), k.reshape(hd), v.reshape(hd), g, beta)
    o = o * lax.rsqrt(jnp.mean(jnp.square(o), axis=-1, keepdims=True) + RMS_EPS)
    o = o * gdn_norm_w.astype(jnp.float32) * jax.nn.silu(z.reshape(hd).astype(jnp.float32))
    o = o.reshape(bsz, seq, GDN_WIDTH).astype(x.dtype)

    sc = gate_b * causal_depthwise_conv(gate_c * h, conv_short)

    return jnp.concatenate([o, sc], axis=-1) @ w_out


def peer_ffn(x, w_q, sub_keys, w_down, w_up):
    bsz, seq, d = x.shape
    t = bsz * seq
    xt = x.reshape(t, d)
    q = (xt @ w_q).reshape(t, PEER_HEADS, 2, PEER_HALF)
    s = jnp.einsum('thpd,hpnd->thpn', q, sub_keys).astype(jnp.float32)
    s_top, i_top = lax.top_k(s, PEER_TOPK)
    cand = s_top[:, :, 0, :, None] + s_top[:, :, 1, None, :]
    cand_s, cand_pos = lax.top_k(cand.reshape(t, PEER_HEADS, PEER_TOPK * PEER_TOPK), PEER_TOPK)
    i1 = jnp.take_along_axis(i_top[:, :, 0], cand_pos // PEER_TOPK, axis=-1)
    i2 = jnp.take_along_axis(i_top[:, :, 1], cand_pos % PEER_TOPK, axis=-1)
    expert = i1 * PEER_NKEYS + i2
    gate = jax.nn.softmax(cand_s, axis=-1).astype(x.dtype)

    nb = t // PEER_BLOCK

    def block(args):
        xb, eb, gb = args
        u = jnp.take(w_down, eb, axis=0)
        a = jax.nn.gelu(jnp.einsum('bhkd,bd->bhk', u, xb), approximate=False) * gb
        v = jnp.take(w_up, eb, axis=0)
        return jnp.einsum('bhk,bhkd->bd', a, v)

    y = lax.map(block, (xt.reshape(nb, PEER_BLOCK, d),
                        expert.reshape(nb, PEER_BLOCK, PEER_HEADS, PEER_TOPK),
                        gate.reshape(nb, PEER_BLOCK, PEER_HEADS, PEER_TOPK)))
    return y.reshape(bsz, seq, d)


def setup_inputs(seed: int = 0) -> dict:
    key = jax.random.key(seed)
    ks = jax.random.split(key, 18)
    f32 = jnp.float32
    L, D = DEPTH, D_MODEL
    nrm = lambda k, shape, scale: jax.random.normal(k, shape, f32) * scale
    dt = jnp.exp(jax.random.uniform(ks[4], (L, GDN_HEADS), f32, math.log(1e-3), math.log(1e-1)))
    return {
        "x": jax.random.normal(ks[0], (BATCH, SEQ, D), f32),
        "w_in": nrm(ks[1], (L, D, IN_COLS), D ** -0.5),
        "conv_qkv": nrm(ks[2], (L, GDN_CONV, 3 * GDN_WIDTH), GDN_CONV ** -0.5),
        "a_log": jnp.log(jax.random.uniform(ks[3], (L, GDN_HEADS), f32, 1.0, 16.0)),
        "dt_bias": dt + jnp.log(-jnp.expm1(-dt)),
        "gdn_norm_w": 1.0 + nrm(ks[5], (L, GDN_HEAD_DIM), 0.01),
        "conv_short": nrm(ks[6], (L, SHORT_CONV, CONV_WIDTH), SHORT_CONV ** -0.5),
        "w_out": nrm(ks[7], (L, MIX_WIDTH, D), DN_BETA * MIX_WIDTH ** -0.5),
        "ln1_g": 1.0 + nrm(ks[8], (L, D), 0.01),
        "ln1_b": nrm(ks[9], (L, D), 0.01),
        "peer_wq": nrm(ks[10], (L, D, PEER_HEADS * PEER_QDIM), D ** -0.5),
        "peer_keys": nrm(ks[11], (L, PEER_HEADS, 2, PEER_NKEYS, PEER_HALF), PEER_HALF ** -0.5),
        "peer_down": nrm(ks[12], (L, PEER_EXPERTS, D), D ** -0.5),
        "peer_up": nrm(ks[13], (L, PEER_EXPERTS, D), DN_BETA),
        "ln2_g": 1.0 + nrm(ks[14], (L, D), 0.01),
        "ln2_b": nrm(ks[15], (L, D), 0.01),
    }


def reference(x, w_in, conv_qkv, a_log, dt_bias, gdn_norm_w, conv_short, w_out,
              ln1_g, ln1_b, peer_wq, peer_keys, peer_down, peer_up, ln2_g, ln2_b):
    for l in range(DEPTH):
        mix = hybrid_mixer(x, w_in[l], conv_qkv[l], a_log[l], dt_bias[l], gdn_norm_w[l],
                           conv_short[l], w_out[l])
        x = layer_norm(DN_ALPHA * x + mix, ln1_g[l], ln1_b[l])
        ffn = peer_ffn(x, peer_wq[l], peer_keys[l], peer_down[l], peer_up[l])
        x = layer_norm(DN_ALPHA * x + ffn, ln2_g[l], ln2_b[l])
    return x
```

```python
import functools
import math

import jax
import jax.numpy as jnp
from jax import lax
from jax.experimental import pallas as pl
from jax.experimental.pallas import tpu as pltpu

F32 = jnp.float32
BF16 = jnp.bfloat16
HIGHEST = lax.Precision.HIGHEST

GDN_HEADS = 16
HEAD_DIM = 128
GDN_WIDTH = GDN_HEADS * HEAD_DIM
GDN_CONV = 4
CHUNK = 64
CONV_WIDTH = 2048
SHORT_CONV = 3
PEER_HEADS = 8
PEER_TOPK = 16
PEER_NKEYS = 128
PEER_HALF = 128
LN_EPS = 1e-5
RMS_EPS = 1e-6

HEADS_PER_GROUP = 4
N_GROUPS = GDN_HEADS // HEADS_PER_GROUP
GROUP_WIDTH = HEADS_PER_GROUP * HEAD_DIM
LANES = 128
SUBLANES = 8

VMEM_LIMIT_BYTES = 58 * 1024 * 1024


def _cparams(*sem):
    return pltpu.CompilerParams(dimension_semantics=sem, vmem_limit_bytes=VMEM_LIMIT_BYTES)


def _tile(n, pref):
    t = min(n, pref)
    assert n % t == 0, (n, t)
    return t


def _dot(a, b, **kw):
    return jnp.dot(a, b, preferred_element_type=F32, **kw)


def _dot_nt(a, b, **kw):
    return lax.dot_general(a, b, (((1,), (1,)), ((), ())), preferred_element_type=F32, **kw)


def _dot_tn(a, b, **kw):
    return lax.dot_general(a, b, (((0,), (0,)), ((), ())), preferred_element_type=F32, **kw)


def _sigmoid(x):
    return 1.0 / (1.0 + jnp.exp(-x))


def _silu(x):
    return x * _sigmoid(x)


def _mm_kernel(a_ref, b_ref, o_ref):
    o_ref[...] = _dot(a_ref[...], b_ref[...]).astype(o_ref.dtype)


def _matmul(a, b, *, tm, tn, out_dtype=F32):
    m, k = a.shape
    _, n = b.shape
    tm, tn = _tile(m, tm), _tile(n, tn)
    return pl.pallas_call(
        _mm_kernel,
        grid=(m // tm, n // tn),
        in_specs=[pl.BlockSpec((tm, k), lambda i, j: (i, 0)),
                  pl.BlockSpec((k, tn), lambda i, j: (0, j))],
        out_specs=pl.BlockSpec((tm, tn), lambda i, j: (i, j)),
        out_shape=jax.ShapeDtypeStruct((m, n), out_dtype),
        compiler_params=_cparams("parallel", "parallel"),
        name="proj_matmul",
    )(a, b)


def _gate_prep_kernel(x_ref, wb_ref, wa_ref, alog_ref, dtb_ref, tri_ref, beta_ref, gcum_ref):
    x = x_ref[...]
    beta = _sigmoid(_dot(x, wb_ref[...]))
    sp_in = _dot(x, wa_ref[...]) + dtb_ref[...]
    softplus = jnp.maximum(sp_in, 0.0) + jnp.log1p(jnp.exp(-jnp.abs(sp_in)))
    g = -jnp.exp(alog_ref[...]) * softplus
    gcum = _dot(tri_ref[...], g, precision=HIGHEST)
    for gi in range(N_GROUPS):
        sl = slice(gi * LANES, (gi + 1) * LANES)
        beta_ref[gi] = beta[:, sl]
        gcum_ref[gi] = gcum[:, sl]


def _gate_prep(xb, wb_g, wa_g, alog_g, dtb_g, *, tp):
    t, d = xb.shape
    tp = _tile(t, tp)
    gl = N_GROUPS * LANES
    row = jnp.arange(tp)[:, None]
    col = jnp.arange(tp)[None, :]
    tri = ((col <= row) & ((row // CHUNK) == (col // CHUNK))).astype(F32)
    out = jax.ShapeDtypeStruct((N_GROUPS, t, LANES), F32)
    ospec = pl.BlockSpec((N_GROUPS, tp, LANES), lambda i: (0, i, 0))
    return pl.pallas_call(
        _gate_prep_kernel,
        grid=(t // tp,),
        in_specs=[pl.BlockSpec((tp, d), lambda i: (i, 0)),
                  pl.BlockSpec((d, gl), lambda i: (0, 0)),
                  pl.BlockSpec((d, gl), lambda i: (0, 0)),
                  pl.BlockSpec((1, gl), lambda i: (0, 0)),
                  pl.BlockSpec((1, gl), lambda i: (0, 0)),
                  pl.BlockSpec((tp, tp), lambda i: (0, 0))],
        out_specs=[ospec, ospec],
        out_shape=[out, out],
        compiler_params=_cparams("parallel"),
        name="gate_prep",
    )(xb, wb_g, wa_g, alog_g, dtb_g, tri)


STACK = HEADS_PER_GROUP * CHUNK
GROUPS_PER_STEP = 4
(MASK_EYE, MASK_DIAG8, MASK_MERGE8, MASK_MERGE16, MASK_MERGE32, MASK_CAUSAL, MASK_STRICT) = range(7)
N_MASKS = 7


def _gdn_masks():
    row = jnp.arange(STACK)[:, None]
    col = jnp.arange(STACK)[None, :]
    same_head = (row // CHUNK) == (col // CHUNK)
    masks = [row == col, (row // 8) == (col // 8)]
    b = 8
    while b < CHUNK:
        masks.append(((row // (2 * b)) == (col // (2 * b)))
                     & (((row // b) % 2) == 1) & (((col // b) % 2) == 0))
        b *= 2
    masks += [same_head & (col <= row), same_head & (col < row)]
    sq = jnp.stack(masks).astype(F32)
    head_of_col = jnp.arange(GROUP_WIDTH)[None, :] // HEAD_DIM
    bd = ((row // CHUNK) == head_of_col).astype(F32)
    return sq, bd, bd.T


def _split_bf16(x):
    hi = x.astype(BF16)
    lo = (x - hi.astype(F32)).astype(BF16)
    return hi, lo


def _dot_split(a, b):
    (ah, al), (bh, bl) = a, b
    return _dot(ah, bh) + (_dot(ah, bl) + _dot(al, bh))


def _each(fn, *lists):
    return [fn(*args) for args in zip(*lists)]


def _unit_lower_inverse(ms, masks_ref):
    eye = masks_ref[MASK_EYE]
    lo16 = lambda x: x.astype(BF16)
    sq = lambda a: _dot(a, a)
    n1 = _each(lambda m: -(m * masks_ref[MASK_DIAG8]), ms)
    n1b = _each(lo16, n1)
    n2 = _each(sq, n1b)
    n2b = _each(lo16, n2)
    n4 = _each(sq, n2b)
    d = _each(lambda a, c: _dot(lo16(eye + a), lo16(eye + c)), n1, n2)
    d = _each(lambda a, c: _dot(lo16(a), lo16(eye + c)), d, n4)
    for idx in (MASK_MERGE8, MASK_MERGE16, MASK_MERGE32):
        db = _each(lo16, d)
        de = _each(lambda a, m: _dot(a, lo16(m * masks_ref[idx])), db, ms)
        d = _each(lambda a, c, ab: a - _dot(lo16(c), ab), d, de, db)
    return d


def _stack_heads(x):
    return jnp.concatenate([x[:, h * HEAD_DIM:(h + 1) * HEAD_DIM] for h in range(HEADS_PER_GROUP)],
                           axis=0)


def _gdn_groups(qs, ks, vs, zs, beta_ref, gcum_ref, nw, masks_ref, bd_ref, bdt_ref, state_ref):
    groups = range(GROUPS_PER_STEP)
    heads = range(HEADS_PER_GROUP)
    lo16 = lambda x: x.astype(BF16)

    def grp(x):
        return [_stack_heads(x[:, gi * GROUP_WIDTH:(gi + 1) * GROUP_WIDTH]) for gi in groups]

    q, k, v, z = grp(qs), grp(ks), grp(vs), grp(zs)
    qn = _each(lambda x: x * lax.rsqrt(jnp.sum(x * x, axis=-1, keepdims=True) + RMS_EPS)
               * (HEAD_DIM ** -0.5), q)
    kn = _each(lambda x: x * lax.rsqrt(jnp.sum(x * x, axis=-1, keepdims=True) + RMS_EPS), k)

    beta_blk = [beta_ref[gi] for gi in groups]
    gcum_blk = [gcum_ref[gi] for gi in groups]
    col = lambda blk: jnp.concatenate([blk[:, h:h + 1] for h in heads], axis=0)
    bcol = _each(col, beta_blk)
    gcol = _each(col, gcum_blk)

    def row(blk):
        rows = blk.T
        return jnp.concatenate([rows[h:h + 1, :] for h in heads], axis=1)

    grow = _each(row, gcum_blk)
    glast = _each(lambda blk: jnp.concatenate(
        [jnp.broadcast_to(blk[CHUNK - 1:CHUNK, h:h + 1], (CHUNK, 1)) for h in heads], axis=0), gcum_blk)
    sdecay = _each(lambda blk: jnp.concatenate(
        [jnp.broadcast_to(jnp.exp(blk[CHUNK - 1:CHUNK, h:h + 1]), (HEAD_DIM, 1)) for h in heads],
        axis=0), gcum_blk)

    causal = masks_ref[MASK_CAUSAL]
    decay = _each(lambda gc, gr: jnp.exp((gc - gr) * causal) * causal, gcol, grow)
    kb = _each(lambda a, c: a * c, kn, bcol)
    vb = _each(lambda a, c: a * c, v, bcol)
    kn16 = _each(lo16, kn)
    m = _each(lambda a, c, dc: _dot_nt(lo16(a), c) * (dc * masks_ref[MASK_STRICT]), kb, kn16, decay)
    qk = _each(lambda a, c, dc: _dot_nt(lo16(a), c) * dc, qn, kn16, decay)

    tinv16 = _each(lo16, _unit_lower_inverse(m, masks_ref))
    eg = _each(jnp.exp, gcol)
    rhs = _each(lambda a, c, e: jnp.concatenate([a, c * e], axis=-1), vb, kb, eg)
    sol0 = _each(lambda t, r: _dot(t, lo16(r)), tinv16, rhs)
    resid = _each(lambda r, s0, mm: r - s0 - _dot_split(_split_bf16(mm), _split_bf16(s0)),
                  rhs, sol0, m)
    sol = _each(lambda s0, t, r: s0 + _dot(t, lo16(r)), sol0, tinv16, resid)
    u = [x[:, :HEAD_DIM] for x in sol]
    w = [x[:, HEAD_DIM:] for x in sol]

    def block_diag(x):
        return lo16(jnp.concatenate([x] * HEADS_PER_GROUP, axis=1) * bd_ref[...])

    s = [state_ref[gi] for gi in groups]
    s16 = _each(lo16, s)
    v_new = _each(lambda a, c, st: a - _dot(block_diag(c), st), u, w, s16)
    v_new16 = _each(lo16, v_new)
    o = _each(lambda a, e, st, c, vn: _dot(block_diag(a * e), st) + _dot(lo16(c), vn),
              qn, eg, s16, qk, v_new16)

    def kdec_block_diag(kk, gl, gc):
        kdec_t = (kk * jnp.exp(gl - gc)).T
        return lo16(jnp.concatenate([kdec_t] * HEADS_PER_GROUP, axis=0) * bdt_ref[...])

    kdec = _each(kdec_block_diag, kn, glast, gcol)
    for gi in groups:
        state_ref[gi] = s[gi] * sdecay[gi] + _dot(kdec[gi], v_new16[gi])

    o = _each(lambda x: x * lax.rsqrt(jnp.mean(x * x, axis=-1, keepdims=True) + RMS_EPS), o)
    return _each(lambda x, zz: x * nw * _silu(zz), o, z)


def _gdn_kernel(q_ref, k_ref, v_ref, z_ref, beta_ref, gcum_ref, wq_ref, wk_ref, wv_ref, nw_ref,
                masks_ref, bd_ref, bdt_ref, o_ref, state_ref, hist_ref, xe_ref):
    c = pl.program_id(2)

    @pl.when(c == 0)
    def _():
        state_ref[...] = jnp.zeros_like(state_ref)
        hist_ref[...] = jnp.zeros_like(hist_ref)

    def conv_silu(x_ref, w_ref, idx):
        xe_ref[0:SUBLANES, :] = hist_ref[idx]
        xe_ref[SUBLANES:SUBLANES + CHUNK, :] = x_ref[...]
        hist_ref[idx] = x_ref[CHUNK - SUBLANES:CHUNK, :]
        y = w_ref[0:1, :] * xe_ref[SUBLANES - 3:SUBLANES - 3 + CHUNK, :]
        for j in range(1, GDN_CONV):
            off = SUBLANES - (GDN_CONV - 1) + j
            y = y + w_ref[j:j + 1, :] * xe_ref[off:off + CHUNK, :]
        return _silu(y)

    qs = conv_silu(q_ref, wq_ref, 0)
    ks = conv_silu(k_ref, wk_ref, 1)
    vs = conv_silu(v_ref, wv_ref, 2)
    outs = _gdn_groups(qs, ks, vs, z_ref[...], beta_ref, gcum_ref, nw_ref[...], masks_ref, bd_ref,
                       bdt_ref, state_ref)
    for gi, o in enumerate(outs):
        for h in range(HEADS_PER_GROUP):
            lo = gi * GROUP_WIDTH + h * HEAD_DIM
            o_ref[:, lo:lo + HEAD_DIM] = o[h * CHUNK:(h + 1) * CHUNK, :].astype(o_ref.dtype)


def _gdn(qkvz, beta_g, gcum_g, conv_w, norm_w, *, batch, seq):
    t = batch * seq
    nc = seq // CHUNK
    sw = GROUPS_PER_STEP * GROUP_WIDTH
    nsteps = GDN_WIDTH // sw
    kq, kk, kv, kz = (i * nsteps for i in range(4))
    masks, bd, bdt = _gdn_masks()

    def xspec(off):
        return pl.BlockSpec((CHUNK, sw), lambda b, g, c: (b * nc + c, off + g))

    def wspec(off):
        return pl.BlockSpec((GDN_CONV, sw), lambda b, g, c: (0, off + g))

    gspec = pl.BlockSpec((GROUPS_PER_STEP, CHUNK, LANES), lambda b, g, c: (g, b * nc + c, 0))
    return pl.pallas_call(
        _gdn_kernel,
        grid=(batch, nsteps, nc),
        in_specs=[xspec(kq), xspec(kk), xspec(kv), xspec(kz), gspec, gspec,
                  wspec(kq), wspec(kk), wspec(kv),
                  pl.BlockSpec((1, HEAD_DIM), lambda b, g, c: (0, 0)),
                  pl.BlockSpec((N_MASKS, STACK, STACK), lambda b, g, c: (0, 0, 0)),
                  pl.BlockSpec((STACK, GROUP_WIDTH), lambda b, g, c: (0, 0)),
                  pl.BlockSpec((GROUP_WIDTH, STACK), lambda b, g, c: (0, 0))],
        out_specs=pl.BlockSpec((CHUNK, sw), lambda b, g, c: (b * nc + c, g)),
        out_shape=jax.ShapeDtypeStruct((t, GDN_WIDTH), BF16),
        scratch_shapes=[pltpu.VMEM((GROUPS_PER_STEP, GROUP_WIDTH, HEAD_DIM), F32),
                        pltpu.VMEM((3, SUBLANES, sw), F32),
                        pltpu.VMEM((SUBLANES + CHUNK, sw), F32)],
        compiler_params=_cparams("parallel", "parallel", "arbitrary"),
        name="gated_delta_rule",
    )(qkvz, qkvz, qkvz, qkvz, beta_g, gcum_g, conv_w, conv_w, conv_w, norm_w, masks, bd, bdt)


def _sconv_kernel(gb_ref, gc_ref, h_ref, w_ref, o_ref, ye_ref):
    ts = gb_ref.shape[0]

    @pl.when(pl.program_id(1) == 0)
    def _():
        ye_ref[0:SUBLANES, :] = jnp.zeros((SUBLANES, ye_ref.shape[1]), F32)

    ye_ref[SUBLANES:SUBLANES + ts, :] = gc_ref[...] * h_ref[...]
    y = w_ref[0:1, :] * ye_ref[SUBLANES - 2:SUBLANES - 2 + ts, :]
    for j in range(1, SHORT_CONV):
        off = SUBLANES - (SHORT_CONV - 1) + j
        y = y + w_ref[j:j + 1, :] * ye_ref[off:off + ts, :]
    o_ref[...] = (gb_ref[...] * y).astype(o_ref.dtype)
    ye_ref[0:SUBLANES, :] = ye_ref[ts:ts + SUBLANES, :]


def _short_conv(bch, conv_w, *, batch, seq, ts):
    t = batch * seq
    ts = _tile(seq, ts)
    ns = seq // ts
    cw = CONV_WIDTH

    def xspec(off):
        return pl.BlockSpec((ts, cw), lambda b, s: (b * ns + s, off))

    return pl.pallas_call(
        _sconv_kernel,
        grid=(batch, ns),
        in_specs=[xspec(0), xspec(1), xspec(2),
                  pl.BlockSpec((SHORT_CONV, cw), lambda b, s: (0, 0))],
        out_specs=pl.BlockSpec((ts, cw), lambda b, s: (b * ns + s, 0)),
        out_shape=jax.ShapeDtypeStruct((t, cw), BF16),
        scratch_shapes=[pltpu.VMEM((SUBLANES + ts, cw), F32)],
        compiler_params=_cparams("parallel", "arbitrary"),
        name="short_conv",
    )(bch, bch, bch, conv_w)


def _layer_norm(x, g, b):
    mu = jnp.mean(x, axis=-1, keepdims=True)
    xc = x - mu
    var = jnp.mean(xc * xc, axis=-1, keepdims=True)
    return xc * lax.rsqrt(var + LN_EPS) * g + b


def _outproj_ln_kernel(o_ref, sc_ref, w_ref, x_ref, g_ref, b_ref, y_ref, y16_ref, *, alpha, k_split):
    k = pl.program_id(1)

    @pl.when(k == 0)
    def _():
        y_ref[...] = jnp.zeros_like(y_ref)

    @pl.when(k < k_split)
    def _():
        y_ref[...] += _dot(o_ref[...], w_ref[...])

    @pl.when(k >= k_split)
    def _():
        y_ref[...] += _dot(sc_ref[...], w_ref[...])

    @pl.when(k == pl.num_programs(1) - 1)
    def _():
        y = _layer_norm(alpha * x_ref[...] + y_ref[...], g_ref[...], b_ref[...])
        y_ref[...] = y
        y16_ref[...] = y.astype(BF16)


def _outproj_ln(o, sc, w_out, x, ln_g, ln_b, *, alpha, tm, tk):
    t, d = x.shape
    tm = _tile(t, tm)
    tk = _tile(o.shape[1], tk)
    ks = o.shape[1] // tk
    nk = w_out.shape[0] // tk
    kern = functools.partial(_outproj_ln_kernel, alpha=alpha, k_split=ks)
    return pl.pallas_call(
        kern,
        grid=(t // tm, nk),
        in_specs=[pl.BlockSpec((tm, tk), lambda i, k: (i, jnp.minimum(k, ks - 1))),
                  pl.BlockSpec((tm, tk), lambda i, k: (i, jnp.maximum(k - ks, 0))),
                  pl.BlockSpec((tk, d), lambda i, k: (k, 0)),
                  pl.BlockSpec((tm, d), lambda i, k: (i, 0)),
                  pl.BlockSpec((1, d), lambda i, k: (0, 0)),
                  pl.BlockSpec((1, d), lambda i, k: (0, 0))],
        out_specs=[pl.BlockSpec((tm, d), lambda i, k: (i, 0)),
                   pl.BlockSpec((tm, d), lambda i, k: (i, 0))],
        out_shape=[jax.ShapeDtypeStruct((t, d), F32), jax.ShapeDtypeStruct((t, d), BF16)],
        compiler_params=_cparams("parallel", "arbitrary"),
        name="outproj_layernorm",
    )(o, sc, w_out, x, ln_g, ln_b)


def _peer_scores_kernel(x_ref, wq_ref, keys_ref, s_ref):
    q = _dot(x_ref[...], wq_ref[...]).astype(BF16)
    for j in range(2):
        qj = q[:, j * PEER_HALF:(j + 1) * PEER_HALF]
        s_ref[j * PEER_NKEYS:(j + 1) * PEER_NKEYS, :] = _dot_nt(keys_ref[j], qj)


def _peer_scores(x16, wq16, keys16, *, tm):
    t, d = x16.shape
    tm = _tile(t, tm)
    nq = wq16.shape[1]
    npair = nq // (2 * PEER_HALF)
    return pl.pallas_call(
        _peer_scores_kernel,
        grid=(t // tm, npair),
        in_specs=[pl.BlockSpec((tm, d), lambda i, j: (i, 0)),
                  pl.BlockSpec((d, 2 * PEER_HALF), lambda i, j: (0, j)),
                  pl.BlockSpec((2, PEER_NKEYS, PEER_HALF), lambda i, j: (j, 0, 0))],
        out_specs=pl.BlockSpec((2 * PEER_NKEYS, tm), lambda i, j: (j, i)),
        out_shape=jax.ShapeDtypeStruct((npair * 2 * PEER_NKEYS, t), F32),
        compiler_params=_cparams("parallel", "parallel"),
        name="peer_scores",
    )(x16, wq16, keys16)


def _top16_rows(x, iota_f):
    n = x.shape[0]
    vals, idxs = [], []
    for _ in range(PEER_TOPK):
        mx = jnp.max(x, axis=0, keepdims=True)
        ix = jnp.min(jnp.where(x == mx, iota_f, float(n)), axis=0, keepdims=True)
        vals.append(mx)
        idxs.append(ix)
        x = jnp.where(iota_f == ix, -jnp.inf, x)
    return jnp.concatenate(vals, axis=0), jnp.concatenate(idxs, axis=0)


def _peer_topk_kernel(s_ref, i1_ref, i2_ref, gate_ref, i1_scr, i2_scr, g_scr):
    tl = s_ref.shape[1]
    kk = PEER_TOPK
    iota_k = lax.broadcasted_iota(jnp.int32, (PEER_NKEYS, tl), 0).astype(F32)
    iota_c = lax.broadcasted_iota(jnp.int32, (kk * kk, tl), 0).astype(F32)

    def head(h, carry):
        base = pl.multiple_of(h * (2 * PEER_NKEYS), 2 * PEER_NKEYS)
        v0, j0 = _top16_rows(s_ref[pl.ds(base, PEER_NKEYS), :], iota_k)
        v1, j1 = _top16_rows(s_ref[pl.ds(base + PEER_NKEYS, PEER_NKEYS), :], iota_k)
        cand = jnp.concatenate([v0[a:a + 1, :] + v1 for a in range(kk)], axis=0)
        cs, pos = _top16_rows(cand, iota_c)
        pa = jnp.floor(pos * (1.0 / kk))
        pb = pos - pa * kk
        i1 = jnp.zeros_like(pos)
        i2 = jnp.zeros_like(pos)
        for a in range(kk):
            i1 = jnp.where(pa == float(a), j0[a:a + 1, :], i1)
            i2 = jnp.where(pb == float(a), j1[a:a + 1, :], i2)
        e = jnp.exp(cs - jnp.max(cs, axis=0, keepdims=True))
        gate = e / jnp.sum(e, axis=0, keepdims=True)
        rows = pl.ds(pl.multiple_of(h * kk, kk), kk)
        i1_scr[rows, :] = i1
        i2_scr[rows, :] = i2
        g_scr[rows, :] = gate
        return carry

    lax.fori_loop(0, PEER_HEADS, head, 0)
    i1_ref[...] = i1_scr[...].T.astype(jnp.int32)
    i2_ref[...] = i2_scr[...].T.astype(jnp.int32)
    gate_ref[...] = g_scr[...].T


def _peer_topk(scores_t, *, tl):
    nrow, t = scores_t.shape
    tl = _tile(t, tl)
    slots = PEER_HEADS * PEER_TOPK
    ospec = pl.BlockSpec((tl, slots), lambda i: (i, 0))
    return pl.pallas_call(
        _peer_topk_kernel,
        grid=(t // tl,),
        in_specs=[pl.BlockSpec((nrow, tl), lambda i: (0, i))],
        out_specs=[ospec, ospec, ospec],
        out_shape=[jax.ShapeDtypeStruct((t, slots), jnp.int32),
                   jax.ShapeDtypeStruct((t, slots), jnp.int32),
                   jax.ShapeDtypeStruct((t, slots), F32)],
        scratch_shapes=[pltpu.VMEM((slots, tl), F32)] * 3,
        compiler_params=_cparams("parallel"),
        name="peer_topk",
    )(scores_t)


GATE_TOKENS_PER_ITER = 8


def _gate_matrix_kernel(i1_ref, i2_ref, gate_ref, g_ref):
    tb = i1_ref.shape[0]
    slots = i1_ref.shape[1]
    iota = lax.broadcasted_iota(jnp.int32, (PEER_NKEYS, slots), 0)

    def one_token(t):
        r1 = i1_ref[pl.ds(t, 1), :]
        r2 = i2_ref[pl.ds(t, 1), :]
        g = gate_ref[pl.ds(t, 1), :]
        g_hi = g.astype(BF16).astype(F32)
        g_lo = g - g_hi
        p = jnp.where(iota == r1, 1.0, 0.0).astype(BF16)
        hit2 = iota == r2
        q_hi = jnp.where(hit2, g_hi, 0.0).astype(BF16)
        q_lo = jnp.where(hit2, g_lo, 0.0).astype(BF16)
        g_ref[t] = _dot_nt(p, q_hi) + _dot_nt(p, q_lo)

    def body(i, carry):
        base = pl.multiple_of(i * GATE_TOKENS_PER_ITER, GATE_TOKENS_PER_ITER)
        for j in range(GATE_TOKENS_PER_ITER):
            one_token(base + j)
        return carry

    lax.fori_loop(0, tb // GATE_TOKENS_PER_ITER, body, 0)


def _gate_matrix(i1, i2, gate, *, tb):
    t, slots = i1.shape
    tb = _tile(t, tb)
    ispec = pl.BlockSpec((tb, slots), lambda i: (i, 0))
    return pl.pallas_call(
        _gate_matrix_kernel,
        grid=(t // tb,),
        in_specs=[ispec, ispec, ispec],
        out_specs=pl.BlockSpec((tb, PEER_NKEYS, PEER_NKEYS), lambda i: (i, 0, 0)),
        out_shape=jax.ShapeDtypeStruct((t, PEER_NKEYS, PEER_NKEYS), F32),
        compiler_params=_cparams("parallel"),
        name="peer_gate_matrix",
    )(i1, i2, gate)


def _gelu(x):
    return 0.5 * x * (1.0 + lax.erf(x * math.sqrt(0.5)))


def _peer_act_kernel(x_ref, wd_ref, g_ref, a_ref):
    tm = x_ref.shape[0]
    nblk = g_ref.shape[1]
    h = _dot_nt(x_ref[...], wd_ref[...])
    for c in range(nblk):
        sl = slice(c * PEER_NKEYS, (c + 1) * PEER_NKEYS)
        a_ref[:, sl] = (_gelu(h[:, sl]) * g_ref[:, c, :]).astype(a_ref.dtype)


def _peer_act(x16, wd16, gmat, *, tm):
    t, d = x16.shape
    e = wd16.shape[0]
    tm = _tile(t, tm)
    nblk = SUBLANES
    te = nblk * PEER_NKEYS
    return pl.pallas_call(
        _peer_act_kernel,
        grid=(t // tm, e // te),
        in_specs=[pl.BlockSpec((tm, d), lambda i, j: (i, 0)),
                  pl.BlockSpec((te, d), lambda i, j: (j, 0)),
                  pl.BlockSpec((tm, nblk, PEER_NKEYS), lambda i, j: (i, j, 0))],
        out_specs=pl.BlockSpec((tm, te), lambda i, j: (i, j)),
        out_shape=jax.ShapeDtypeStruct((t, e), BF16),
        compiler_params=_cparams("parallel", "parallel"),
        name="peer_activations",
    )(x16, wd16, gmat)


def _peer_up_ln_kernel(a_ref, w_ref, x_ref, g_ref, b_ref, y_ref, *, alpha):
    k = pl.program_id(1)

    @pl.when(k == 0)
    def _():
        y_ref[...] = jnp.zeros_like(y_ref)

    y_ref[...] += _dot(a_ref[...], w_ref[...])

    @pl.when(k == pl.num_programs(1) - 1)
    def _():
        y_ref[...] = _layer_norm(alpha * x_ref[...] + y_ref[...], g_ref[...], b_ref[...])


def _peer_up_ln(a16, wu16, x1, ln_g, ln_b, *, alpha, tm, tk):
    t, e = a16.shape
    d = wu16.shape[1]
    tm, tk = _tile(t, tm), _tile(e, tk)
    kern = functools.partial(_peer_up_ln_kernel, alpha=alpha)
    return pl.pallas_call(
        kern,
        grid=(t // tm, e // tk),
        in_specs=[pl.BlockSpec((tm, tk), lambda i, k: (i, k)),
                  pl.BlockSpec((tk, d), lambda i, k: (k, 0)),
                  pl.BlockSpec((tm, d), lambda i, k: (i, 0)),
                  pl.BlockSpec((1, d), lambda i, k: (0, 0)),
                  pl.BlockSpec((1, d), lambda i, k: (0, 0))],
        out_specs=pl.BlockSpec((tm, d), lambda i, k: (i, 0)),
        out_shape=jax.ShapeDtypeStruct((t, d), F32),
        compiler_params=_cparams("parallel", "arbitrary"),
        name="peer_up_layernorm",
    )(a16, wu16, x1, ln_g, ln_b)


def _group_pad(cols):
    r = cols.shape[0]
    w = cols.reshape(r, N_GROUPS, HEADS_PER_GROUP)
    w = jnp.pad(w, ((0, 0), (0, 0), (0, LANES - HEADS_PER_GROUP)))
    return w.reshape(r, N_GROUPS * LANES)


def _layer(x2, batch, seq, alpha, w_in, conv_qkv, a_log, dt_bias, gdn_norm_w, conv_short, w_out,
           ln1_g, ln1_b, peer_wq, peer_keys, peer_down, peer_up, ln2_g, ln2_b):
    t, d = x2.shape
    c_ba = 4 * GDN_WIDTH
    c_conv = c_ba + 2 * GDN_HEADS
    x16 = x2.astype(BF16)

    qkvz = _matmul(x16, w_in[:, :c_ba].astype(BF16), tm=1024, tn=512)
    bch = _matmul(x16, w_in[:, c_conv:].astype(BF16), tm=1024, tn=512)

    wb_g = _group_pad(w_in[:, c_ba:c_ba + GDN_HEADS]).astype(BF16)
    wa_g = _group_pad(w_in[:, c_ba + GDN_HEADS:c_conv]).astype(BF16)
    alog_g = _group_pad(a_log.reshape(1, GDN_HEADS).astype(F32))
    dtb_g = _group_pad(dt_bias.reshape(1, GDN_HEADS).astype(F32))
    beta_g, gcum_g = _gate_prep(x16, wb_g, wa_g, alog_g, dtb_g, tp=256)

    o = _gdn(qkvz, beta_g, gcum_g, conv_qkv, gdn_norm_w.reshape(1, HEAD_DIM), batch=batch, seq=seq)
    sc = _short_conv(bch, conv_short, batch=batch, seq=seq, ts=256)

    x1, x1_16 = _outproj_ln(o, sc, w_out.astype(BF16), x2, ln1_g.reshape(1, d), ln1_b.reshape(1, d),
                            alpha=alpha, tm=512, tk=256)

    keys16 = peer_keys.reshape(PEER_HEADS * 2, PEER_NKEYS, PEER_HALF).astype(BF16)
    scores_t = _peer_scores(x1_16, peer_wq.astype(BF16), keys16, tm=1024)
    i1, i2, gate = _peer_topk(scores_t, tl=128)
    gmat = _gate_matrix(i1, i2, gate, tb=64)

    act = _peer_act(x1_16, peer_down.astype(BF16), gmat, tm=1024)
    return _peer_up_ln(act, peer_up.astype(BF16), x1, ln2_g.reshape(1, d), ln2_b.reshape(1, d),
                       alpha=alpha, tm=512, tk=512)


def kernel(x, w_in, conv_qkv, a_log, dt_bias, gdn_norm_w, conv_short, w_out, ln1_g, ln1_b, peer_wq,
           peer_keys, peer_down, peer_up, ln2_g, ln2_b):
    batch, seq, d = x.shape
    depth = w_in.shape[0]
    alpha = (2.0 * depth) ** 0.25
    x2 = x.reshape(batch * seq, d)
    for l in range(depth):
        x2 = _layer(x2, batch, seq, alpha, w_in[l], conv_qkv[l], a_log[l], dt_bias[l], gdn_norm_w[l],
                    conv_short[l], w_out[l], ln1_g[l], ln1_b[l], peer_wq[l], peer_keys[l],
                    peer_down[l], peer_up[l], ln2_g[l], ln2_b[l])
    return x2.reshape(batch, seq, d)
```
